```python
import jax, jax.numpy as jnp
from jax import lax
import numpy as np

D_MODEL = 1024
BATCH = 8
SEQ = 2048
DEPTH = 2

CHUNK = 64

D_A = D_MODEL
CONV_A_WIDTH = 31
D_B = D_MODEL
CONV_B_WIDTH = 3
D_C = D_MODEL
POOL_WINDOWS = (2, 4, 8, 16)
N_POOL_GROUPS = len(POOL_WINDOWS)
GROUP_C = D_C // N_POOL_GROUPS
N_BRANCHES = 3
IN_COLS = 2 * D_A + 3 * D_B + D_C + N_BRANCHES * D_MODEL

N_EXPERTS = 16
N_EXPERT_GROUPS = 4
EXPERTS_PER_GROUP = N_EXPERTS // N_EXPERT_GROUPS
TOP_K = 2
D_EXPERT = D_MODEL // 2

LN_EPS = 1e-5
DEEPNORM_ALPHA = (2 * DEPTH) ** 0.25
DEEPNORM_BETA = (8 * DEPTH) ** -0.25

kernel_name = "hybrid_conv_pool_gated_moe_trunk"


def layer_norm(x, g, b):
    x32 = x.astype(jnp.float32)
    mu = jnp.mean(x32, axis=-1, keepdims=True)
    xc = x32 - mu
    var = jnp.mean(xc * xc, axis=-1, keepdims=True)
    y = xc * lax.rsqrt(var + LN_EPS) * g.astype(jnp.float32) + b.astype(jnp.float32)
    return y.astype(x.dtype)


def causal_depthwise_conv(u, w):
    k, c = w.shape
    return lax.conv_general_dilated(
        u, w[:, None, :].astype(u.dtype), window_strides=(1,), padding=[(k - 1, 0)],
        dimension_numbers=("NWC", "WIO", "NWC"), feature_group_count=c)


def multi_scale_pool_minus_self(u):
    t_len = u.shape[1]
    u32 = u.astype(jnp.float32)
    cs = jnp.cumsum(u32, axis=1)
    t_idx = jnp.arange(t_len)
    outs = []
    for g, w in enumerate(POOL_WINDOWS):
        c = cs[..., g * GROUP_C:(g + 1) * GROUP_C]
        lower = jnp.pad(c, ((0, 0), (w, 0), (0, 0)))[:, :t_len]
        count = jnp.minimum(t_idx + 1, w).astype(jnp.float32)[None, :, None]
        outs.append((c - lower) / count - u32[..., g * GROUP_C:(g + 1) * GROUP_C])
    return jnp.concatenate(outs, axis=-1).astype(u.dtype)


def hybrid_mixer(x, w_in, b_in, conv_a_w, conv_a_b, ln_a_g, ln_a_b, w_a_out,
                 conv_b_w, w_b_out, w_c_group, c_scale, w_o, b_o):
    bsz, t_len, _ = x.shape
    proj = jnp.einsum("btd,dc->btc", x, w_in) + b_in
    o1 = 2 * D_A
    o2 = o1 + 3 * D_B
    o3 = o2 + D_C
    a_in, b_part, c_in, gate_in = jnp.split(proj, [o1, o2, o3], axis=-1)

    a_val, a_gate = jnp.split(a_in, 2, axis=-1)
    a = a_val * jax.nn.sigmoid(a_gate)
    a = causal_depthwise_conv(a, conv_a_w) + conv_a_b
    a = jax.nn.silu(layer_norm(a, ln_a_g, ln_a_b))
    y_a = jnp.einsum("btc,cd->btd", a, w_a_out)

    gate_bb, gate_cc, xb = jnp.split(b_part, 3, axis=-1)
    yb = gate_bb * causal_depthwise_conv(gate_cc * xb, conv_b_w)
    y_b = jnp.einsum("btc,cd->btd", yb, w_b_out)

    pooled = multi_scale_pool_minus_self(c_in).reshape(bsz, t_len, N_POOL_GROUPS, GROUP_C)
    y_c = jnp.einsum("btgi,gio->btgo", pooled, w_c_group).reshape(bsz, t_len, D_C) * c_scale

    g = jax.nn.sigmoid(gate_in).reshape(bsz, t_len, N_BRANCHES, D_MODEL)
    merged = g[..., 0, :] * y_a + g[..., 1, :] * y_b + g[..., 2, :] * y_c
    return jnp.einsum("btd,de->bte", merged, w_o) + b_o


def grouped_moe(x, w_router, b_router, w_exp_gate, w_exp_up, w_exp_down):
    bsz, t_len, d = x.shape
    xt = x.reshape(-1, d)
    logits = (xt @ w_router + b_router).astype(jnp.float32)
    probs = jax.nn.softmax(logits, axis=-1)
    grouped = probs.reshape(-1, N_EXPERT_GROUPS, EXPERTS_PER_GROUP)
    group_score = lax.top_k(grouped, TOP_K)[0].sum(-1)
    best_group = jnp.argmax(group_score, axis=-1)
    in_group = jnp.einsum("ng,nge->ne", jax.nn.one_hot(best_group, N_EXPERT_GROUPS, dtype=jnp.float32), grouped)
    top_p, top_i = lax.top_k(in_group, TOP_K)
    top_p = top_p / jnp.sum(top_p, axis=-1, keepdims=True)
    expert_idx = best_group[:, None] * EXPERTS_PER_GROUP + top_i
    combine = jnp.einsum("nk,nke->ne", top_p, jax.nn.one_hot(expert_idx, N_EXPERTS, dtype=jnp.float32)).astype(x.dtype)
    y = jnp.zeros_like(xt)
    for e in range(N_EXPERTS):
        h = jax.nn.silu(xt @ w_exp_gate[e]) * (xt @ w_exp_up[e])
        y = y + combine[:, e:e + 1] * (h @ w_exp_down[e])
    return y.reshape(bsz, t_len, d)


def setup_inputs(seed: int = 0) -> dict:
    key = jax.random.key(seed)
    ks = jax.random.split(key, 24)
    f32 = jnp.float32

    def nrm(k, shape, scale):
        return jax.random.normal(k, shape, f32) * scale

    L = DEPTH
    return {
        "x": nrm(ks[0], (BATCH, SEQ, D_MODEL), 1.0),
        "w_in": nrm(ks[1], (L, D_MODEL, IN_COLS), D_MODEL ** -0.5),
        "b_in": nrm(ks[2], (L, IN_COLS), 0.01),
        "conv_a_w": nrm(ks[3], (L, CONV_A_WIDTH, D_A), CONV_A_WIDTH ** -0.5),
        "conv_a_b": nrm(ks[4], (L, D_A), 0.01),
        "ln_a_g": 1.0 + nrm(ks[5], (L, D_A), 0.01),
        "ln_a_b": nrm(ks[6], (L, D_A), 0.01),
        "w_a_out": nrm(ks[7], (L, D_A, D_MODEL), D_A ** -0.5 * DEEPNORM_BETA),
        "conv_b_w": nrm(ks[8], (L, CONV_B_WIDTH, D_B), CONV_B_WIDTH ** -0.5),
        "w_b_out": nrm(ks[9], (L, D_B, D_MODEL), D_B ** -0.5 * DEEPNORM_BETA),
        "w_c_group": nrm(ks[10], (L, N_POOL_GROUPS, GROUP_C, GROUP_C), GROUP_C ** -0.5 * DEEPNORM_BETA),
        "c_scale": 1.0 + nrm(ks[11], (L, D_C), 0.01),
        "w_o": nrm(ks[12], (L, D_MODEL, D_MODEL), D_MODEL ** -0.5 * DEEPNORM_BETA),
        "b_o": nrm(ks[13], (L, D_MODEL), 0.01),
        "ln1_g": 1.0 + nrm(ks[14], (L, D_MODEL), 0.01),
        "ln1_b": nrm(ks[15], (L, D_MODEL), 0.01),
        "w_router": nrm(ks[16], (D_MODEL, N_EXPERTS), D_MODEL ** -0.5),
        "b_router": nrm(ks[17], (N_EXPERTS,), 0.01),
        "w_exp_gate": nrm(ks[18], (L, N_EXPERTS, D_MODEL, D_EXPERT), D_MODEL ** -0.5),
        "w_exp_up": nrm(ks[19], (L, N_EXPERTS, D_MODEL, D_EXPERT), D_MODEL ** -0.5),
        "w_exp_down": nrm(ks[20], (L, N_EXPERTS, D_EXPERT, D_MODEL), D_EXPERT ** -0.5 * DEEPNORM_BETA),
        "ln2_g": 1.0 + nrm(ks[21], (L, D_MODEL), 0.01),
        "ln2_b": nrm(ks[22], (L, D_MODEL), 0.01),
    }


def reference(x, w_in, b_in, conv_a_w, conv_a_b, ln_a_g, ln_a_b, w_a_out,
              conv_b_w, w_b_out, w_c_group, c_scale, w_o, b_o, ln1_g, ln1_b,
              w_router, b_router, w_exp_gate, w_exp_up, w_exp_down, ln2_g, ln2_b):
    for l in range(DEPTH):
        mix = hybrid_mixer(x, w_in[l], b_in[l], conv_a_w[l], conv_a_b[l], ln_a_g[l], ln_a_b[l],
                           w_a_out[l], conv_b_w[l], w_b_out[l], w_c_group[l], c_scale[l],
                           w_o[l], b_o[l])
        x = layer_norm(DEEPNORM_ALPHA * x + mix, ln1_g[l], ln1_b[l])
        ffn = grouped_moe(x, w_router, b_router, w_exp_gate[l], w_exp_up[l], w_exp_down[l])
        x = layer_norm(DEEPNORM_ALPHA * x + ffn, ln2_g[l], ln2_b[l])
    return x
```

```python
import functools

import jax
import jax.numpy as jnp
from jax import lax
from jax.experimental import pallas as pl
from jax.experimental.pallas import tpu as pltpu

CONV_A_WIDTH = 31
CONV_B_WIDTH = 3
POOL_WINDOWS = (2, 4, 8, 16)
N_EXPERTS = 16
N_EXPERT_GROUPS = 4
EXPERTS_PER_GROUP = N_EXPERTS // N_EXPERT_GROUPS
LN_EPS = 1e-5
EXPERT_PAIRS = ((0, 1), (0, 2), (0, 3), (1, 2), (1, 3), (2, 3))
N_BUCKETS = N_EXPERT_GROUPS * len(EXPERT_PAIRS)

V7X_LANES = 128
V7X_SUBLANES = 8
V7X_VMEM_BYTES = 64 * 1024 * 1024
VMEM_LIMIT_BYTES = V7X_VMEM_BYTES - 8 * 1024 * 1024

MIX_ROWS = 256
MOE_ROWS = 256
PERM_ROWS = 512
A_HALO = 32
B_HALO = V7X_SUBLANES
C_HALO = 16
BUCKET_ROWS = 32
INFO_LANES = V7X_LANES

_F32 = jnp.float32
_BF16 = jnp.bfloat16


def _sigmoid(v):
    return 1.0 / (1.0 + jnp.exp(-v))


def _layer_norm(v, g, b):
    mu = jnp.mean(v, axis=-1, keepdims=True)
    vc = v - mu
    var = jnp.mean(vc * vc, axis=-1, keepdims=True)
    return vc * lax.rsqrt(var + LN_EPS) * g + b


def _dot(a, b):
    return jnp.dot(a, b, preferred_element_type=_F32)


def _dot_nt(a, b):
    return lax.dot_general(a, b, (((1,), (1,)), ((), ())),
                           preferred_element_type=_F32)


def _route(logits):
    m = jnp.max(logits, axis=0, keepdims=True)
    e = jnp.exp(logits - m)
    probs = e / jnp.sum(e, axis=0, keepdims=True)
    rows = [probs[k:k + 1, :] for k in range(N_EXPERTS)]

    best = None
    best_score = None
    for g in range(N_EXPERT_GROUPS):
        v = rows[EXPERTS_PER_GROUP * g:EXPERTS_PER_GROUP * (g + 1)]
        score = None
        for (i, j) in EXPERT_PAIRS:
            s = v[i] + v[j]
            score = s if score is None else jnp.maximum(score, s)
        if best is None:
            best = jnp.zeros_like(score, dtype=jnp.int32)
            best_score = score
        else:
            upd = score > best_score
            best = jnp.where(upd, g, best)
            best_score = jnp.where(upd, score, best_score)

    w = []
    for k in range(EXPERTS_PER_GROUP):
        wk = rows[k]
        for g in range(1, N_EXPERT_GROUPS):
            wk = jnp.where(best == g, rows[EXPERTS_PER_GROUP * g + k], wk)
        w.append(wk)

    def first_max(vals):
        mx = vals[0]
        for v in vals[1:]:
            mx = jnp.maximum(mx, v)
        taken = None
        sel = []
        for v in vals:
            hit = v == mx
            if taken is None:
                sel.append(hit)
                taken = hit
            else:
                sel.append(hit & ~taken)
                taken = taken | hit
        return mx, sel

    m1, sel1 = first_max(w)
    w_rest = [jnp.where(s, -1.0, v) for s, v in zip(sel1, w)]
    m2, sel2 = first_max(w_rest)
    sel = [a | b for a, b in zip(sel1, sel2)]
    denom = m1 + m2

    lo = jnp.where(sel[0], 0, jnp.where(sel[1], 1, jnp.where(sel[2], 2, 3)))
    hi = jnp.where(sel[3], 3, jnp.where(sel[2], 2, jnp.where(sel[1], 1, 0)))
    pair = jnp.where(lo == 0, hi - 1, jnp.where(lo == 1, hi + 1, 5))
    w_lo = jnp.where(lo == 0, w[0], jnp.where(lo == 1, w[1], w[2]))
    w_hi = jnp.where(hi == 3, w[3], jnp.where(hi == 2, w[2], w[1]))
    bucket = best * len(EXPERT_PAIRS) + pair
    return bucket.astype(jnp.int32), w_lo / denom, w_hi / denom


def _mixer_body(x_ref, w_in_ref, b_in_ref, caw_ref, cab_ref, lag_ref, lab_ref,
                wao_ref, cbw_ref, wbo_ref, wc_ref, cs_ref, wo_ref, bo_ref,
                l1g_ref, l1b_ref, wrh_ref, wrl_ref, br_ref,
                x1e_ref, route_ref, cnt_ref,
                abuf, vbuf, ubuf, cnt_acc, *, alpha):
    tt, d = x_ref.shape
    b_id = pl.program_id(0)
    t_id = pl.program_id(1)

    @pl.when(t_id == 0)
    def _():
        abuf[0:A_HALO, :] = jnp.zeros((A_HALO, d), _F32)
        vbuf[0:B_HALO, :] = jnp.zeros((B_HALO, d), _F32)
        ubuf[0:C_HALO, :] = jnp.zeros((C_HALO, d), _F32)

    @pl.when((t_id == 0) & (b_id == 0))
    def _():
        cnt_acc[...] = jnp.zeros(cnt_acc.shape, _F32)

    x = x_ref[...]
    xb = x.astype(_BF16)

    def proj(c0, c1):
        return _dot(xb, w_in_ref[:, c0:c1]) + b_in_ref[:, c0:c1]

    pa = proj(0, 2 * d)
    abuf[A_HALO:A_HALO + tt, :] = pa[:, :d] * _sigmoid(pa[:, d:])
    a0 = A_HALO - (CONV_A_WIDTH - 1)
    conv = caw_ref[0:1, :] * abuf[pl.ds(a0, tt), :]
    for k in range(1, CONV_A_WIDTH):
        conv = conv + caw_ref[k:k + 1, :] * abuf[pl.ds(a0 + k, tt), :]
    abuf[0:A_HALO, :] = abuf[tt:tt + A_HALO, :]
    an = _layer_norm(conv + cab_ref[...], lag_ref[...], lab_ref[...])
    y_a = _dot((an * _sigmoid(an)).astype(_BF16), wao_ref[...])

    pb = proj(2 * d, 5 * d)
    vbuf[B_HALO:B_HALO + tt, :] = pb[:, d:2 * d] * pb[:, 2 * d:]
    b0 = B_HALO - (CONV_B_WIDTH - 1)
    convb = cbw_ref[0:1, :] * vbuf[pl.ds(b0, tt), :]
    for k in range(1, CONV_B_WIDTH):
        convb = convb + cbw_ref[k:k + 1, :] * vbuf[pl.ds(b0 + k, tt), :]
    vbuf[0:B_HALO, :] = vbuf[tt:tt + B_HALO, :]
    y_b = _dot((pb[:, :d] * convb).astype(_BF16), wbo_ref[...])

    ubuf[C_HALO:C_HALO + tt, :] = proj(5 * d, 6 * d)
    t_glob = t_id * tt + lax.broadcasted_iota(jnp.int32, (tt, 1), 0)
    gc = d // len(POOL_WINDOWS)
    y_c_parts = []
    for g, w in enumerate(POOL_WINDOWS):
        c0 = g * gc
        cur = ubuf[pl.ds(C_HALO, tt), c0:c0 + gc]
        tot = cur
        for i in range(1, w):
            tot = tot + ubuf[pl.ds(C_HALO - i, tt), c0:c0 + gc]
        count = jnp.minimum(t_glob + 1, w).astype(_F32)
        pooled = tot / count - cur
        y_c_parts.append(_dot(pooled.astype(_BF16), wc_ref[g]))
    ubuf[0:C_HALO, :] = ubuf[tt:tt + C_HALO, :]
    y_c = jnp.concatenate(y_c_parts, axis=-1) * cs_ref[...]

    gates = _sigmoid(proj(6 * d, 9 * d))
    merged = (gates[:, :d] * y_a + gates[:, d:2 * d] * y_b
              + gates[:, 2 * d:] * y_c)
    mix = _dot(merged.astype(_BF16), wo_ref[...]) + bo_ref[...]
    x1 = _layer_norm(alpha * x + mix, l1g_ref[...], l1b_ref[...])

    x1_hi = x1.astype(_BF16)
    x1_lo = (x1 - x1_hi.astype(_F32)).astype(_BF16)
    logits = (_dot_nt(wrh_ref[...], x1_hi) + _dot_nt(wrh_ref[...], x1_lo)
              + _dot_nt(wrl_ref[...], x1_hi)) + br_ref[...]
    bucket, p_lo, p_hi = _route(logits)

    brow = lax.broadcasted_iota(jnp.int32, (BUCKET_ROWS, tt), 0)
    onehot = (brow == bucket).astype(_F32)
    src = lax.broadcasted_iota(jnp.int32, (tt, tt), 0)
    dst = lax.broadcasted_iota(jnp.int32, (tt, tt), 1)
    earlier = (src < dst).astype(_BF16)
    before = _dot(onehot.astype(_BF16), earlier) + cnt_acc[:, 0:1]
    rank = jnp.sum(onehot * before, axis=0, keepdims=True).astype(jnp.int32)
    cnt_acc[...] = cnt_acc[...] + jnp.sum(onehot, axis=1, keepdims=True)
    cnt_ref[...] = cnt_acc[...]

    rrow = lax.broadcasted_iota(jnp.int32, (V7X_SUBLANES, tt), 0)
    route_ref[...] = jnp.where(rrow == 0, bucket,
                               jnp.where(rrow == 1, rank, 0))

    prow = lax.broadcasted_iota(jnp.int32, (INFO_LANES, tt), 0)
    info = jnp.where(prow == 0, p_lo, jnp.where(prow == 1, p_hi, 0.0))
    x1e_ref[:, 0:d] = x1
    x1e_ref[:, d:d + INFO_LANES] = info.T


def _resident(shape):
    zeros = (0,) * len(shape)
    return pl.BlockSpec(shape, lambda b, t: zeros,
                        pipeline_mode=pl.Buffered(1))


def _mixer(x, p, alpha):
    bsz, t_len, d = x.shape
    tt = MIX_ROWS
    nt = t_len // tt
    n = bsz * t_len
    weights = [p["w_in"], p["b_in"], p["conv_a_w"], p["conv_a_b"], p["ln_a_g"],
               p["ln_a_b"], p["w_a_out"], p["conv_b_w"], p["w_b_out"],
               p["w_c_group"], p["c_scale"], p["w_o"], p["b_o"], p["ln1_g"],
               p["ln1_b"], p["w_r_hi"], p["w_r_lo"], p["b_r"]]
    in_specs = [pl.BlockSpec((None, tt, d), lambda b, t: (b, t, 0))]
    in_specs += [_resident(w.shape) for w in weights]
    out_shape = (jax.ShapeDtypeStruct((n, d + INFO_LANES), _F32),
                 jax.ShapeDtypeStruct((V7X_SUBLANES, n), jnp.int32),
                 jax.ShapeDtypeStruct((BUCKET_ROWS, V7X_LANES), _F32))
    out_specs = (pl.BlockSpec((tt, d + INFO_LANES), lambda b, t: (b * nt + t, 0)),
                 pl.BlockSpec((V7X_SUBLANES, tt), lambda b, t: (0, b * nt + t)),
                 pl.BlockSpec((BUCKET_ROWS, V7X_LANES), lambda b, t: (0, 0)))
    scratch = [pltpu.VMEM((A_HALO + tt, d), _F32),
               pltpu.VMEM((B_HALO + tt, d), _F32),
               pltpu.VMEM((C_HALO + tt, d), _F32),
               pltpu.VMEM((BUCKET_ROWS, V7X_LANES), _F32)]
    return pl.pallas_call(
        functools.partial(_mixer_body, alpha=alpha),
        grid=(bsz, nt),
        in_specs=in_specs,
        out_specs=out_specs,
        out_shape=out_shape,
        scratch_shapes=scratch,
        compiler_params=pltpu.CompilerParams(
            dimension_semantics=("arbitrary", "arbitrary"),
            vmem_limit_bytes=VMEM_LIMIT_BYTES),
        name="mixer",
    )(x, *weights)


def _plan(counts, bucket, rank, n_tiles):
    tm = MOE_ROWS
    tiles = (counts + tm - 1) // tm
    tile_end = jnp.cumsum(tiles)
    tile_start = tile_end - tiles
    total = tile_end[-1]
    offs = tile_start * tm
    pos = offs[bucket] + rank

    ti = jnp.arange(n_tiles, dtype=jnp.int32)
    used = ti < total
    ti_eff = jnp.where(used, ti, total - 1)
    bkt = jnp.minimum(jnp.searchsorted(tile_end, ti_eff, side="right"),
                      N_BUCKETS - 1).astype(jnp.int32)
    lo = jnp.array([q[0] for q in EXPERT_PAIRS], jnp.int32)
    hi = jnp.array([q[1] for q in EXPERT_PAIRS], jnp.int32)
    grp = bkt // len(EXPERT_PAIRS)
    e_lo = EXPERTS_PER_GROUP * grp + lo[bkt % len(EXPERT_PAIRS)]
    e_hi = EXPERTS_PER_GROUP * grp + hi[bkt % len(EXPERT_PAIRS)]

    fill_lo = jnp.concatenate([offs + counts, (total * tm)[None]])
    fill_hi = jnp.concatenate([tile_end * tm,
                               jnp.array([n_tiles * tm], jnp.int32)])
    return (pos.astype(jnp.int32), e_lo.astype(jnp.int32),
            e_hi.astype(jnp.int32), used.astype(jnp.int32),
            ti_eff.astype(jnp.int32), fill_lo.astype(jnp.int32),
            fill_hi.astype(jnp.int32))


def _row_copy(src, src_row, dst, dst_row, sem):
    return pltpu.make_async_copy(src.at[pl.ds(src_row, 1), :],
                                 dst.at[pl.ds(dst_row, 1), :], sem)


def _scatter_body(fill_lo_ref, fill_hi_ref, pos_ref, x_ref, xs_ref,
                  zrow, sem, zsem, *, n_fill_rows):
    rows = x_ref.shape[0]

    def issue(r, carry):
        _row_copy(x_ref, r, xs_ref, pos_ref[0, 0, r], sem).start()
        return carry
    lax.fori_loop(0, rows, issue, 0, unroll=8)

    @pl.when(pl.program_id(0) == 0)
    def _():
        zrow[...] = jnp.zeros(zrow.shape, _F32)

        def fill(r, carry):
            _row_copy(zrow, 0, xs_ref, r, zsem).start()
            return carry
        for j in range(N_BUCKETS + 1):
            lax.fori_loop(fill_lo_ref[j], fill_hi_ref[j], fill, 0)

        def drain(r, carry):
            _row_copy(zrow, 0, xs_ref, 0, zsem).wait()
            return carry
        lax.fori_loop(0, n_fill_rows, drain, 0, unroll=8)

    for r in range(rows):
        _row_copy(x_ref, r, xs_ref, 0, sem).wait()


def _scatter(x1e, pos, fill_lo, fill_hi, n_tiles):
    n, de = x1e.shape
    ts = PERM_ROWS
    n_sorted = n_tiles * MOE_ROWS
    grid_spec = pltpu.PrefetchScalarGridSpec(
        num_scalar_prefetch=2,
        grid=(n // ts,),
        in_specs=[pl.BlockSpec((1, 1, ts), lambda i, *_: (i, 0, 0),
                               memory_space=pltpu.SMEM),
                  pl.BlockSpec((ts, de), lambda i, *_: (i, 0))],
        out_specs=pl.BlockSpec(memory_space=pl.ANY),
        scratch_shapes=[pltpu.VMEM((V7X_SUBLANES, de), _F32),
                        pltpu.SemaphoreType.DMA(()),
                        pltpu.SemaphoreType.DMA(())])
    return pl.pallas_call(
        functools.partial(_scatter_body, n_fill_rows=n_sorted - n),
        grid_spec=grid_spec,
        out_shape=jax.ShapeDtypeStruct((n_sorted, de), _F32),
        compiler_params=pltpu.CompilerParams(
            dimension_semantics=("arbitrary",),
            vmem_limit_bytes=VMEM_LIMIT_BYTES),
        name="scatter",
    )(fill_lo, fill_hi, pos.reshape(n // ts, 1, ts), x1e)


def _expert_body(e_lo_ref, e_hi_ref, used_ref, xblk_ref, xs_ref,
                 wg_lo, wu_lo, wd_lo, wg_hi, wu_hi, wd_hi, ys_ref, *, alpha):
    del e_lo_ref, e_hi_ref, xblk_ref
    d = ys_ref.shape[1]
    i = pl.program_id(0)

    @pl.when(used_ref[i] == 1)
    def _():
        x = xs_ref[:, 0:d]
        p_lo = xs_ref[:, d:d + 1]
        p_hi = xs_ref[:, d + 1:d + 2]
        xb = x.astype(_BF16)

        def ffn(wg, wu, wd):
            g = _dot(xb, wg[...])
            u = _dot(xb, wu[...])
            h = (g * _sigmoid(g)) * u
            return _dot(h.astype(_BF16), wd[...])

        y = p_lo * ffn(wg_lo, wu_lo, wd_lo) + p_hi * ffn(wg_hi, wu_hi, wd_hi)
        ys_ref[...] = alpha * x + y

    @pl.when(used_ref[i] == 0)
    def _():
        ys_ref[...] = jnp.zeros(ys_ref.shape, _F32)


def _experts(xs, e_lo, e_hi, used, xblk, wg, wu, wd, alpha):
    n_sorted, de = xs.shape
    d = de - INFO_LANES
    f = wg.shape[2]
    tm = MOE_ROWS

    def w_spec(shape, which):
        if which == 0:
            return pl.BlockSpec(shape, lambda i, lo, hi, us, xb: (lo[i], 0, 0))
        return pl.BlockSpec(shape, lambda i, lo, hi, us, xb: (hi[i], 0, 0))

    grid_spec = pltpu.PrefetchScalarGridSpec(
        num_scalar_prefetch=4,
        grid=(n_sorted // tm,),
        in_specs=[pl.BlockSpec((tm, de), lambda i, lo, hi, us, xb: (xb[i], 0)),
                  w_spec((None, d, f), 0), w_spec((None, d, f), 0),
                  w_spec((None, f, d), 0),
                  w_spec((None, d, f), 1), w_spec((None, d, f), 1),
                  w_spec((None, f, d), 1)],
        out_specs=pl.BlockSpec((tm, d), lambda i, lo, hi, us, xb: (i, 0)))
    return pl.pallas_call(
        functools.partial(_expert_body, alpha=alpha),
        grid_spec=grid_spec,
        out_shape=jax.ShapeDtypeStruct((n_sorted, d), _F32),
        compiler_params=pltpu.CompilerParams(
            dimension_semantics=("arbitrary",),
            vmem_limit_bytes=VMEM_LIMIT_BYTES),
        name="experts",
    )(e_lo, e_hi, used, xblk, xs, wg, wu, wd, wg, wu, wd)


def _gather_body(pos_ref, ys_ref, g_ref, b_ref, o_ref, buf, sem):
    rows = o_ref.shape[0]

    def issue(r, carry):
        _row_copy(ys_ref, pos_ref[0, 0, r], buf, r, sem).start()
        return carry
    lax.fori_loop(0, rows, issue, 0, unroll=8)
    for r in range(rows):
        _row_copy(ys_ref, 0, buf, r, sem).wait()
    o_ref[...] = _layer_norm(buf[...], g_ref[...], b_ref[...])


def _gather_norm(ys, pos, g, b, n):
    d = ys.shape[1]
    tg = PERM_ROWS
    return pl.pallas_call(
        _gather_body,
        grid=(n // tg,),
        in_specs=[pl.BlockSpec((1, 1, tg), lambda i: (i, 0, 0),
                               memory_space=pltpu.SMEM),
                  pl.BlockSpec(memory_space=pl.ANY),
                  pl.BlockSpec((1, d), lambda i: (0, 0)),
                  pl.BlockSpec((1, d), lambda i: (0, 0))],
        out_specs=pl.BlockSpec((tg, d), lambda i: (i, 0)),
        out_shape=jax.ShapeDtypeStruct((n, d), _F32),
        scratch_shapes=[pltpu.VMEM((tg, d), _F32),
                        pltpu.SemaphoreType.DMA(())],
        compiler_params=pltpu.CompilerParams(
            dimension_semantics=("arbitrary",),
            vmem_limit_bytes=VMEM_LIMIT_BYTES),
        name="gather_norm",
    )(pos.reshape(n // tg, 1, tg), ys, g, b)


def _row(v):
    return v.reshape(1, -1).astype(_F32)


def kernel(x, w_in, b_in, conv_a_w, conv_a_b, ln_a_g, ln_a_b, w_a_out, conv_b_w, w_b_out, w_c_group, c_scale, w_o, b_o, ln1_g, ln1_b, w_router, b_router, w_exp_gate, w_exp_up, w_exp_down, ln2_g, ln2_b):
    depth = w_in.shape[0]
    bsz, t_len, d = x.shape
    n = bsz * t_len
    assert t_len % MIX_ROWS == 0 and n % PERM_ROWS == 0 and n % MOE_ROWS == 0
    assert w_router.shape == (d, N_EXPERTS)
    alpha = float((2 * depth) ** 0.25)
    n_tiles = n // MOE_ROWS + N_BUCKETS

    w_r = w_router.T.astype(_F32)
    w_r_hi = w_r.astype(_BF16)
    w_r_lo = (w_r - w_r_hi.astype(_F32)).astype(_BF16)
    b_r = b_router.reshape(N_EXPERTS, 1).astype(_F32)

    for l in range(depth):
        params = dict(
            w_in=w_in[l].astype(_BF16), b_in=_row(b_in[l]),
            conv_a_w=conv_a_w[l].astype(_F32), conv_a_b=_row(conv_a_b[l]),
            ln_a_g=_row(ln_a_g[l]), ln_a_b=_row(ln_a_b[l]),
            w_a_out=w_a_out[l].astype(_BF16),
            conv_b_w=conv_b_w[l].astype(_F32),
            w_b_out=w_b_out[l].astype(_BF16),
            w_c_group=w_c_group[l].astype(_BF16), c_scale=_row(c_scale[l]),
            w_o=w_o[l].astype(_BF16), b_o=_row(b_o[l]),
            ln1_g=_row(ln1_g[l]), ln1_b=_row(ln1_b[l]),
            w_r_hi=w_r_hi, w_r_lo=w_r_lo, b_r=b_r)
        x1e, route, cnt = _mixer(x, params, alpha)
        counts = cnt[:N_BUCKETS, 0].astype(jnp.int32)
        pos, e_lo, e_hi, used, xblk, fill_lo, fill_hi = _plan(
            counts, route[0], route[1], n_tiles)
        xs = _scatter(x1e, pos, fill_lo, fill_hi, n_tiles)
        ys = _experts(xs, e_lo, e_hi, used, xblk,
                      w_exp_gate[l].astype(_BF16), w_exp_up[l].astype(_BF16),
                      w_exp_down[l].astype(_BF16), alpha)
        x = _gather_norm(ys, pos, _row(ln2_g[l]), _row(ln2_b[l]), n)
        x = x.reshape(bsz, t_len, d)
    return x
```
